```python
import math
import jax, jax.numpy as jnp
from jax import lax
import numpy as np

D_MODEL = 1024
BATCH = 8
SEQ = 4096
DEPTH = 4

N_MIXERS = 2
N_DIFF_LAYERS = (DEPTH + 1) // 2
N_SWA_LAYERS = DEPTH // 2

DIFF_HEADS = 8
DIFF_HEAD_DIM = D_MODEL // DIFF_HEADS // 2
DIFF_Q_BLOCK = 128

SWA_Q_HEADS = 16
SWA_KV_HEADS = 4
SWA_GROUP = SWA_Q_HEADS // SWA_KV_HEADS
SWA_HEAD_DIM = D_MODEL // SWA_Q_HEADS
WINDOW = 128
SWA_BLOCK = WINDOW

ROPE_DIM = 64
ROPE_THETA = 10000.0

D_FF = 2816
CONV_WIDTH = 3
PLE_DIM = 256
EPS = 1e-6

kernel_name = "hybrid_diffattn_swa_convffn_ple_encoder"


def rms_norm(x, g):
    xf = x.astype(jnp.float32)
    y = xf * lax.rsqrt(jnp.mean(xf * xf, axis=-1, keepdims=True) + EPS)
    return (y * g.astype(jnp.float32)).astype(x.dtype)


def rope_tables(positions):
    inv_freq = ROPE_THETA ** (-jnp.arange(0, ROPE_DIM, 2, dtype=jnp.float32) / ROPE_DIM)
    ang = positions.astype(jnp.float32)[..., None] * inv_freq
    return jnp.cos(ang), jnp.sin(ang)


def apply_rope(x, cos, sin):
    xf = x.astype(jnp.float32)
    x1, x2 = jnp.split(xf, 2, axis=-1)
    c, s = cos[:, :, None, :], sin[:, :, None, :]
    return jnp.concatenate([x1 * c - x2 * s, x2 * c + x1 * s], axis=-1).astype(x.dtype)


def diff_attention(h, w_qkv, w_o, lam, subln, cos, sin, layer_idx):
    B, S, _ = h.shape
    H, d = DIFF_HEADS, DIFF_HEAD_DIM
    q, k, v = jnp.split(h @ w_qkv, 3, axis=-1)
    q = apply_rope(q.reshape(B, S, 2 * H, d), cos, sin)
    k = apply_rope(k.reshape(B, S, 2 * H, d), cos, sin)
    v = v.reshape(B, S, H, 2 * d)
    lam_init = 0.8 - 0.6 * math.exp(-0.3 * layer_idx)
    lf = lam.astype(jnp.float32)
    lam_full = jnp.exp(jnp.sum(lf[0] * lf[1])) - jnp.exp(jnp.sum(lf[2] * lf[3])) + lam_init
    scale = d ** -0.5
    nb = S // DIFF_Q_BLOCK
    q_blocks = jnp.moveaxis(q.reshape(B, nb, DIFF_Q_BLOCK, 2 * H, d), 1, 0)

    def one_block(qb):
        s = jnp.einsum('bqhd,bkhd->bhqk', qb, k).astype(jnp.float32) * scale
        pr = jax.nn.softmax(s, axis=-1).reshape(B, H, 2, DIFF_Q_BLOCK, S)
        a = (pr[:, :, 0] - lam_full * pr[:, :, 1]).astype(v.dtype)
        return jnp.einsum('bhqk,bkhe->bqhe', a, v)

    o = lax.map(one_block, q_blocks)
    o = jnp.moveaxis(o, 0, 1).reshape(B, S, H, 2 * d)
    o = rms_norm(o, subln) * (1.0 - lam_init)
    return o.reshape(B, S, H * 2 * d) @ w_o


def windowed_gqa(h, w_qkv, w_o, sinks, cos, sin):
    B, S, _ = h.shape
    Hq, Hkv, G, d, L = SWA_Q_HEADS, SWA_KV_HEADS, SWA_GROUP, SWA_HEAD_DIM, SWA_BLOCK
    nb = S // L
    q, k, v = jnp.split(h @ w_qkv, [Hq * d, Hq * d + Hkv * d], axis=-1)
    q = apply_rope(q.reshape(B, S, Hq, d), cos, sin).reshape(B, nb, L, Hkv, G, d)
    k = apply_rope(k.reshape(B, S, Hkv, d), cos, sin)
    v = v.reshape(B, S, Hkv, d)

    def band(t):
        tp = jnp.pad(t, ((0, 0), (L, L), (0, 0), (0, 0))).reshape(B, nb + 2, L, Hkv, d)
        return jnp.concatenate([tp[:, :-2], tp[:, 1:-1], tp[:, 2:]], axis=2)

    kb, vb = band(k), band(v)
    s = jnp.einsum('bnqhgd,bnchd->bnhgqc', q, kb).astype(jnp.float32) * (d ** -0.5)
    blk = jnp.arange(nb)[:, None, None] * L
    qpos = blk + jnp.arange(L)[None, :, None]
    kpos = blk - L + jnp.arange(3 * L)[None, None, :]
    mask = (jnp.abs(kpos - qpos) <= WINDOW) & (kpos >= 0) & (kpos < S)
    s = jnp.where(mask[None, :, None, None], s, -jnp.inf)
    sink = sinks.astype(jnp.float32).reshape(1, 1, Hkv, G, 1, 1)
    m = jnp.maximum(jnp.max(s, axis=-1, keepdims=True), sink)
    e = jnp.exp(s - m)
    pr = e / (jnp.sum(e, axis=-1, keepdims=True) + jnp.exp(sink - m))
    o = jnp.einsum('bnhgqc,bnchd->bnqhgd', pr.astype(v.dtype), vb).reshape(B, S, Hq * d)
    return o @ w_o


def conv_ffn(h, w_up, conv_w, conv_b, w_down):
    S = h.shape[1]
    u = h @ w_up
    pad = CONV_WIDTH // 2
    up = jnp.pad(u, ((0, 0), (pad, pad), (0, 0)))
    u = sum(up[:, t:t + S] * conv_w[t] for t in range(CONV_WIDTH)) + conv_b
    gate, val = jnp.split(u, 2, axis=-1)
    return (jax.nn.silu(gate) * val) @ w_down


def setup_inputs(seed: int = 0) -> dict:
    key = jax.random.key(seed)
    ks = jax.random.split(key, 24)
    D = D_MODEL

    def nrm(k, shape, scale):
        return jax.random.normal(k, shape, jnp.float32) * scale

    return {
        "x": nrm(ks[0], (BATCH, SEQ, D), 1.0),
        "p": nrm(ks[1], (DEPTH, BATCH, SEQ, PLE_DIM), 1.0),
        "positions": jnp.broadcast_to(jnp.arange(SEQ, dtype=jnp.int32), (BATCH, SEQ)),
        "attn_norm": 1.0 + nrm(ks[2], (DEPTH, D), 0.05),
        "ffn_norm": 1.0 + nrm(ks[3], (DEPTH, D), 0.05),
        "ple_norm": 1.0 + nrm(ks[4], (DEPTH, D), 0.05),
        "final_norm": 1.0 + nrm(ks[5], (D,), 0.05),
        "diff_w_qkv": nrm(ks[6], (N_DIFF_LAYERS, D, 3 * D), D ** -0.5),
        "diff_w_o": nrm(ks[7], (N_DIFF_LAYERS, D, D), D ** -0.5),
        "diff_lambda": nrm(ks[8], (N_DIFF_LAYERS, 4, DIFF_HEAD_DIM), 0.1),
        "diff_subln": 1.0 + nrm(ks[9], (N_DIFF_LAYERS, 2 * DIFF_HEAD_DIM), 0.05),
        "swa_w_qkv": nrm(ks[10], (N_SWA_LAYERS, D, (SWA_Q_HEADS + 2 * SWA_KV_HEADS) * SWA_HEAD_DIM), D ** -0.5),
        "swa_w_o": nrm(ks[11], (N_SWA_LAYERS, SWA_Q_HEADS * SWA_HEAD_DIM, D), D ** -0.5),
        "swa_sinks": nrm(ks[12], (N_SWA_LAYERS, SWA_Q_HEADS), 0.5),
        "ffn_w_up": nrm(ks[13], (DEPTH, D, 2 * D_FF), D ** -0.5),
        "ffn_conv_w": nrm(ks[14], (DEPTH, CONV_WIDTH, 2 * D_FF), CONV_WIDTH ** -0.5),
        "ffn_conv_b": nrm(ks[15], (DEPTH, 2 * D_FF), 0.02),
        "ffn_w_down": nrm(ks[16], (DEPTH, D_FF, D), D_FF ** -0.5),
        "ple_w_proj": nrm(ks[17], (DEPTH, PLE_DIM, D), PLE_DIM ** -0.5),
        "ple_w_gate": nrm(ks[18], (DEPTH, D, D), D ** -0.5),
    }


def reference(x, p, positions, attn_norm, ffn_norm, ple_norm, final_norm,
              diff_w_qkv, diff_w_o, diff_lambda, diff_subln,
              swa_w_qkv, swa_w_o, swa_sinks,
              ffn_w_up, ffn_conv_w, ffn_conv_b, ffn_w_down,
              ple_w_proj, ple_w_gate):
    cos, sin = rope_tables(positions)
    h = x
    for i in range(DEPTH):
        j = i // N_MIXERS
        hn = rms_norm(h, attn_norm[i])
        if i % N_MIXERS == 0:
            mix = diff_attention(hn, diff_w_qkv[j], diff_w_o[j], diff_lambda[j], diff_subln[j], cos, sin, i)
        else:
            mix = windowed_gqa(hn, swa_w_qkv[j], swa_w_o[j], swa_sinks[j], cos, sin)
        h = h + mix
        h = h + conv_ffn(rms_norm(h, ffn_norm[i]), ffn_w_up[i], ffn_conv_w[i], ffn_conv_b[i], ffn_w_down[i])
        gate = jax.nn.sigmoid(rms_norm(h, ple_norm[i]) @ ple_w_gate[i])
        h = h + (p[i] @ ple_w_proj[i]) * gate
    return rms_norm(h, final_norm)
```

```python
import functools
import math

import numpy as np
import jax
import jax.numpy as jnp
from jax import lax
from jax.experimental import pallas as pl
from jax.experimental.pallas import tpu as pltpu

F32 = jnp.float32
BF16 = jnp.bfloat16

LANES = 128
HEAD_DIM = 64
HALF = HEAD_DIM // 2
DIFF_HEADS = 8
SWA_Q_HEADS = 16
SWA_KV_HEADS = 4
WINDOW = 128
ROPE_THETA = 10000.0
CONV_WIDTH = 3
EPS = 1e-6
LOG2E = math.log2(math.e)
VMEM_LIMIT_BYTES = 56 * 1024 * 1024


def _params(n_axes):
    return pltpu.CompilerParams(
        dimension_semantics=("arbitrary",) * n_axes,
        vmem_limit_bytes=VMEM_LIMIT_BYTES,
    )


def _resident(shape, index_map):
    return pl.BlockSpec(shape, index_map, pipeline_mode=pl.Buffered(1))


def _rms(x, g):
    return x * lax.rsqrt(jnp.mean(x * x, axis=-1, keepdims=True) + EPS) * g


def _pair_perm(n_heads):
    idx = []
    for j in range(n_heads // 2):
        a, b = 2 * j * HEAD_DIM, (2 * j + 1) * HEAD_DIM
        idx += list(range(a, a + HALF)) + list(range(b, b + HALF))
        idx += list(range(a + HALF, a + HEAD_DIM)) + list(range(b + HALF, b + HEAD_DIM))
    return np.asarray(idx, np.int32)


def _rope_table_kernel(pos_ref, invf_ref, cos_ref, sin_ref):
    ang = pos_ref[...].astype(F32) * invf_ref[...]
    lane = lax.broadcasted_iota(jnp.int32, (1, LANES), 1)
    sign = jnp.where(lane < 2 * HALF, -1.0, 1.0).astype(F32)
    cos_ref[...] = jnp.cos(ang)
    sin_ref[...] = jnp.sin(ang) * sign


def _rope_tables(positions, tm=2048):
    m = positions.size
    tm = min(tm, m)
    inv_freq = ROPE_THETA ** (-jnp.arange(0, HEAD_DIM, 2, dtype=F32) / HEAD_DIM)
    invf = jnp.tile(inv_freq, LANES // HALF).reshape(1, LANES)
    pos = positions.reshape(m, 1)
    return pl.pallas_call(
        _rope_table_kernel,
        grid=(m // tm,),
        in_specs=[pl.BlockSpec((tm, 1), lambda i: (i, 0)),
                  pl.BlockSpec((1, LANES), lambda i: (0, 0))],
        out_specs=[pl.BlockSpec((tm, LANES), lambda i: (i, 0))] * 2,
        out_shape=[jax.ShapeDtypeStruct((m, LANES), F32)] * 2,
        compiler_params=_params(1),
        name="rope_tables",
    )(pos, invf)


def _rope(y, cos, sin):
    return y * cos + pltpu.roll(y, 2 * HALF, axis=1) * sin


def _diff_qkv_kernel(x_ref, g_ref, w_ref, cos_ref, sin_ref, q_ref, k_ref, v_ref, *, d, cw, qscale):
    xn = _rms(x_ref[...], g_ref[...]).astype(BF16)
    cos, sin = cos_ref[...], sin_ref[...]
    for c in range(d // cw):
        sl = slice(c * cw, (c + 1) * cw)
        yq = jnp.dot(xn, w_ref[:, c * cw:(c + 1) * cw], preferred_element_type=F32)
        yk = jnp.dot(xn, w_ref[:, d + c * cw:d + (c + 1) * cw], preferred_element_type=F32)
        for j in range(cw // LANES):
            js = slice(j * LANES, (j + 1) * LANES)
            q_ref[:, c * cw + j * LANES:c * cw + (j + 1) * LANES] = (
                _rope(yq[:, js], cos, sin) * qscale).astype(BF16)
            k_ref[:, c * cw + j * LANES:c * cw + (j + 1) * LANES] = _rope(yk[:, js], cos, sin).astype(BF16)
        v_ref[:, sl] = jnp.dot(xn, w_ref[:, 2 * d + c * cw:2 * d + (c + 1) * cw],
                               preferred_element_type=F32).astype(BF16)


def _diff_qkv(h, g, w, cos, sin, tm=512, cw=512):
    m, d = h.shape
    tm = min(tm, m)
    kern = functools.partial(_diff_qkv_kernel, d=d, cw=cw, qscale=HEAD_DIM ** -0.5 * LOG2E)
    row = lambda i: (i, 0)
    return pl.pallas_call(
        kern,
        grid=(m // tm,),
        in_specs=[pl.BlockSpec((tm, d), row),
                  _resident((1, d), lambda i: (0, 0)),
                  _resident((d, 3 * d), lambda i: (0, 0)),
                  pl.BlockSpec((tm, LANES), row),
                  pl.BlockSpec((tm, LANES), row)],
        out_specs=[pl.BlockSpec((tm, d), row)] * 3,
        out_shape=[jax.ShapeDtypeStruct((m, d), BF16)] * 3,
        compiler_params=_params(1),
        name="diff_qkv",
    )(h, g, w, cos, sin)


def _swa_qkv_kernel(x_ref, g_ref, w_ref, cos_ref, sin_ref, q_ref, ka_ref, kb_ref, va_ref, vb_ref,
                    *, d, cw, qscale):
    xn = _rms(x_ref[...], g_ref[...]).astype(BF16)
    cos, sin = cos_ref[...], sin_ref[...]
    dq = SWA_Q_HEADS * HEAD_DIM
    dkv = SWA_KV_HEADS * HEAD_DIM
    for c in range(dq // cw):
        yq = jnp.dot(xn, w_ref[:, c * cw:(c + 1) * cw], preferred_element_type=F32)
        for j in range(cw // LANES):
            q_ref[:, c * cw + j * LANES:c * cw + (j + 1) * LANES] = (
                _rope(yq[:, j * LANES:(j + 1) * LANES], cos, sin) * qscale).astype(BF16)
    lane = lax.broadcasted_iota(jnp.int32, (1, LANES), 1)
    slot_a = (lane % HEAD_DIM) < HALF
    low = lane < HEAD_DIM
    yk = jnp.dot(xn, w_ref[:, dq:dq + dkv], preferred_element_type=F32)
    yv = jnp.dot(xn, w_ref[:, dq + dkv:dq + 2 * dkv], preferred_element_type=F32)
    for j in range(dkv // LANES):
        kblk = _rope(yk[:, j * LANES:(j + 1) * LANES], cos, sin)
        vblk = yv[:, j * LANES:(j + 1) * LANES]
        k_even = jnp.where(slot_a, kblk, 0.0)
        k_odd = jnp.where(slot_a, 0.0, kblk)
        v_even = jnp.where(low, vblk, 0.0)
        v_odd = jnp.where(low, 0.0, vblk)
        e, o = slice(2 * j * LANES, (2 * j + 1) * LANES), slice((2 * j + 1) * LANES, (2 * j + 2) * LANES)
        ka_ref[:, e] = k_even.astype(BF16)
        kb_ref[:, e] = pltpu.roll(k_even, HALF, axis=1).astype(BF16)
        kb_ref[:, o] = k_odd.astype(BF16)
        ka_ref[:, o] = pltpu.roll(k_odd, LANES - HALF, axis=1).astype(BF16)
        va_ref[:, e] = v_even.astype(BF16)
        vb_ref[:, e] = pltpu.roll(v_even, HEAD_DIM, axis=1).astype(BF16)
        vb_ref[:, o] = v_odd.astype(BF16)
        va_ref[:, o] = pltpu.roll(v_odd, HEAD_DIM, axis=1).astype(BF16)


def _swa_qkv(h, g, w, cos, sin, tm=512, cw=512):
    m, d = h.shape
    tm = min(tm, m)
    dq = SWA_Q_HEADS * HEAD_DIM
    n = w.shape[1]
    kvw = SWA_KV_HEADS * LANES
    kern = functools.partial(_swa_qkv_kernel, d=d, cw=cw, qscale=HEAD_DIM ** -0.5 * LOG2E)
    row = lambda i: (i, 0)
    return pl.pallas_call(
        kern,
        grid=(m // tm,),
        in_specs=[pl.BlockSpec((tm, d), row),
                  _resident((1, d), lambda i: (0, 0)),
                  _resident((d, n), lambda i: (0, 0)),
                  pl.BlockSpec((tm, LANES), row),
                  pl.BlockSpec((tm, LANES), row)],
        out_specs=[pl.BlockSpec((tm, dq), row)] + [pl.BlockSpec((tm, kvw), row)] * 4,
        out_shape=[jax.ShapeDtypeStruct((m, dq), BF16)] + [jax.ShapeDtypeStruct((m, kvw), BF16)] * 4,
        compiler_params=_params(1),
        name="swa_qkv",
    )(h, g, w, cos, sin)


def _diff_attn_kernel(q_ref, k_ref, v_ref, lam_ref, sub_ref, o_ref, m_ref, l_ref, acc_ref,
                      *, tq, tk, seq, lam_init):
    lane = lax.broadcasted_iota(jnp.int32, (1, LANES), 1)
    slot_a = (lane % HEAD_DIM) < HALF
    q = q_ref[0]
    zero = jnp.zeros_like(q)
    q2 = jnp.concatenate([jnp.where(slot_a, q, zero), jnp.where(slot_a, zero, q)], axis=0)
    m_ref[...] = jnp.full(m_ref.shape, -jnp.inf, F32)
    l_ref[...] = jnp.zeros(l_ref.shape, F32)
    acc_ref[...] = jnp.zeros(acc_ref.shape, F32)

    def step(kc, carry):
        start = pl.multiple_of(kc * tk, tk)
        k = k_ref[0, pl.ds(start, tk), :]
        v = v_ref[0, pl.ds(start, tk), :]
        s = lax.dot_general(q2, k, (((1,), (1,)), ((), ())), preferred_element_type=F32)
        m_old = m_ref[...]
        m_new = jnp.maximum(m_old, jnp.max(s, axis=1, keepdims=True))
        p = jnp.exp2(s - m_new)
        alpha = jnp.exp2(m_old - m_new)
        l_ref[...] = alpha * l_ref[...] + jnp.sum(p, axis=1, keepdims=True)
        acc_ref[...] = alpha * acc_ref[...] + jnp.dot(p.astype(BF16), v, preferred_element_type=F32)
        m_ref[...] = m_new
        return carry

    lax.fori_loop(0, seq // tk, step, 0)

    lam = lam_ref[...]
    lam_full = (jnp.exp(jnp.sum(lam[0:1] * lam[1:2], axis=1, keepdims=True))
                - jnp.exp(jnp.sum(lam[2:3] * lam[3:4], axis=1, keepdims=True)) + lam_init)
    o_all = acc_ref[...] / l_ref[...]
    o = o_all[:tq] - lam_full * o_all[tq:]
    o = _rms(o, sub_ref[...]) * (1.0 - lam_init)
    o_ref[0] = o.astype(o_ref.dtype)


def _diff_attention(q, k, v, lam, subln, lam_init, tq=512, tk=512):
    b, s, d = q.shape
    tq, tk = min(tq, s), min(tk, s)
    kern = functools.partial(_diff_attn_kernel, tq=tq, tk=tk, seq=s, lam_init=lam_init)
    return pl.pallas_call(
        kern,
        grid=(b, d // LANES, s // tq),
        in_specs=[pl.BlockSpec((1, tq, LANES), lambda bi, h, i: (bi, i, h)),
                  pl.BlockSpec((1, s, LANES), lambda bi, h, i: (bi, 0, h)),
                  pl.BlockSpec((1, s, LANES), lambda bi, h, i: (bi, 0, h)),
                  _resident((8, LANES), lambda bi, h, i: (0, 0)),
                  _resident((1, LANES), lambda bi, h, i: (0, 0))],
        out_specs=pl.BlockSpec((1, tq, LANES), lambda bi, h, i: (bi, i, h)),
        out_shape=jax.ShapeDtypeStruct((b, s, d), BF16),
        scratch_shapes=[pltpu.VMEM((2 * tq, 1), F32), pltpu.VMEM((2 * tq, 1), F32),
                        pltpu.VMEM((2 * tq, LANES), F32)],
        compiler_params=_params(3),
        name="diff_attention",
    )(q, k, v, lam, subln)


def _swa_attn_kernel(q_ref, ka_ref, kb_ref, va_ref, vb_ref, sink_ref, o_ref, *, tq, band, seq):
    i = pl.program_id(1)
    start = jnp.clip(i * tq - WINDOW, 0, seq - band)
    start = pl.multiple_of(start, WINDOW)
    qpos = i * tq + lax.broadcasted_iota(jnp.int32, (tq, band), 0)
    kpos = start + lax.broadcasted_iota(jnp.int32, (tq, band), 1)
    mask = jnp.abs(kpos - qpos) <= WINDOW
    mask2 = jnp.concatenate([mask, mask], axis=0)
    group = SWA_Q_HEADS // SWA_KV_HEADS
    for g in range(SWA_KV_HEADS):
        gl = slice(g * LANES, (g + 1) * LANES)
        blocks = [q_ref[0, :, (g * group // 2 + t) * LANES:(g * group // 2 + t + 1) * LANES]
                  for t in range(group // 2)]
        qg = jnp.concatenate(blocks, axis=0)
        outs = []
        for kk_ref, vv_ref, which in ((ka_ref, va_ref, 0), (kb_ref, vb_ref, 1)):
            kk = kk_ref[0, pl.ds(start, band), gl]
            vv = vv_ref[0, pl.ds(start, band), gl]
            s = lax.dot_general(qg, kk, (((1,), (1,)), ((), ())), preferred_element_type=F32)
            s = jnp.where(mask2, s, -jnp.inf)
            sink = jnp.concatenate(
                [jnp.full((tq, 1), 1.0, F32) * sink_ref[0:1, g * group + 2 * t + which:g * group + 2 * t + which + 1]
                 for t in range(group // 2)], axis=0)
            m = jnp.maximum(jnp.max(s, axis=1, keepdims=True), sink)
            e = jnp.exp2(s - m)
            denom = jnp.sum(e, axis=1, keepdims=True) + jnp.exp2(sink - m)
            pr = (e / denom).astype(BF16)
            outs.append(jnp.dot(pr, vv, preferred_element_type=F32))
        og = outs[0] + outs[1]
        for t in range(group // 2):
            blk = g * group // 2 + t
            o_ref[0, :, blk * LANES:(blk + 1) * LANES] = og[t * tq:(t + 1) * tq].astype(o_ref.dtype)


def _swa_attention(q, ka, kb, va, vb, sinks, tq=128):
    b, s, dq = q.shape
    kvw = ka.shape[-1]
    tq = min(tq, s)
    band = min(tq + 2 * WINDOW, s)
    kern = functools.partial(_swa_attn_kernel, tq=tq, band=band, seq=s)
    full = pl.BlockSpec((1, s, kvw), lambda bi, i: (bi, 0, 0))
    return pl.pallas_call(
        kern,
        grid=(b, s // tq),
        in_specs=[pl.BlockSpec((1, tq, dq), lambda bi, i: (bi, i, 0)), full, full, full, full,
                  _resident((1, LANES), lambda bi, i: (0, 0))],
        out_specs=pl.BlockSpec((1, tq, dq), lambda bi, i: (bi, i, 0)),
        out_shape=jax.ShapeDtypeStruct((b, s, dq), BF16),
        compiler_params=_params(2),
        name="swa_attention",
    )(q, ka, kb, va, vb, sinks)


def _oproj_kernel(o_ref, w_ref, h_ref, out_ref):
    out_ref[...] = h_ref[...] + jnp.dot(o_ref[...], w_ref[...], preferred_element_type=F32)


def _oproj(o, w, h, tm=1024):
    m, d = h.shape
    tm = min(tm, m)
    row = lambda i: (i, 0)
    return pl.pallas_call(
        _oproj_kernel,
        grid=(m // tm,),
        in_specs=[pl.BlockSpec((tm, o.shape[1]), row),
                  _resident(w.shape, lambda i: (0, 0)),
                  pl.BlockSpec((tm, d), row)],
        out_specs=pl.BlockSpec((tm, d), row),
        out_shape=jax.ShapeDtypeStruct((m, d), F32),
        compiler_params=_params(1),
        name="out_proj",
    )(o, w, h)


def _ffn_kernel(x_ref, xp_ref, xn_ref, g_ref, wup_ref, cwb_ref, wdn_ref, out_ref, *, tm, halo, seq, tf, dff):
    i = pl.program_id(0)
    g = g_ref[...]
    x = x_ref[...]
    row0 = i * tm
    prev_ok = (row0 % seq) != 0
    next_ok = ((row0 + tm) % seq) != 0
    hp = jnp.where(prev_ok, _rms(xp_ref[...], g), 0.0)
    hn = jnp.where(next_ok, _rms(xn_ref[...], g), 0.0)
    xe = jnp.concatenate([hp, _rms(x, g), hn], axis=0).astype(BF16)
    rows = tm + 2 * halo

    def conv(u, col):
        w = cwb_ref[:, col:col + tf]
        prev = pltpu.roll(u, 1, axis=0)[halo:halo + tm]
        nxt = pltpu.roll(u, rows - 1, axis=0)[halo:halo + tm]
        return prev * w[0:1] + u[halo:halo + tm] * w[1:2] + nxt * w[2:3] + w[3:4]

    acc = jnp.zeros((tm, x.shape[1]), F32)
    for c in range(dff // tf):
        ug = jnp.dot(xe, wup_ref[:, c * tf:(c + 1) * tf], preferred_element_type=F32)
        uv = jnp.dot(xe, wup_ref[:, dff + c * tf:dff + (c + 1) * tf], preferred_element_type=F32)
        gate = conv(ug, c * tf)
        val = conv(uv, dff + c * tf)
        act = (gate * jax.nn.sigmoid(gate) * val).astype(BF16)
        acc = acc + jnp.dot(act, wdn_ref[c * tf:(c + 1) * tf, :], preferred_element_type=F32)
    out_ref[...] = x + acc


def _ffn(h, g, w_up, cwb, w_down, seq, tm=512, tf=256):
    m, d = h.shape
    tm = min(tm, seq)
    halo = 8
    dff = w_down.shape[0]
    nhb = m // halo
    kern = functools.partial(_ffn_kernel, tm=tm, halo=halo, seq=seq, tf=tf, dff=dff)
    row = lambda i: (i, 0)
    return pl.pallas_call(
        kern,
        grid=(m // tm,),
        in_specs=[pl.BlockSpec((tm, d), row),
                  pl.BlockSpec((halo, d), lambda i: (jnp.maximum(i * (tm // halo) - 1, 0), 0)),
                  pl.BlockSpec((halo, d), lambda i: (jnp.minimum((i + 1) * (tm // halo), nhb - 1), 0)),
                  _resident((1, d), lambda i: (0, 0)),
                  _resident(w_up.shape, lambda i: (0, 0)),
                  _resident(cwb.shape, lambda i: (0, 0)),
                  _resident(w_down.shape, lambda i: (0, 0))],
        out_specs=pl.BlockSpec((tm, d), row),
        out_shape=jax.ShapeDtypeStruct((m, d), F32),
        compiler_params=_params(1),
        name="conv_ffn",
    )(h, h, h, g, w_up, cwb, w_down)


def _ple_kernel(h_ref, p_ref, g_ref, wg_ref, wp_ref, gf_ref, out_ref, *, final):
    h = h_ref[...]
    hn = _rms(h, g_ref[...]).astype(BF16)
    gate = jax.nn.sigmoid(jnp.dot(hn, wg_ref[...], preferred_element_type=F32))
    proj = jnp.dot(p_ref[0].astype(BF16), wp_ref[...], preferred_element_type=F32)
    out = h + proj * gate
    if final:
        out = _rms(out, gf_ref[...])
    out_ref[...] = out


def _ple(h, p, layer, g, w_gate, w_proj, g_final, final, tm=1024):
    m, d = h.shape
    tm = min(tm, m)
    row = lambda i: (i, 0)
    return pl.pallas_call(
        functools.partial(_ple_kernel, final=final),
        grid=(m // tm,),
        in_specs=[pl.BlockSpec((tm, d), row),
                  pl.BlockSpec((1, tm, p.shape[-1]), lambda i: (layer, i, 0)),
                  _resident((1, d), lambda i: (0, 0)),
                  _resident(w_gate.shape, lambda i: (0, 0)),
                  _resident(w_proj.shape, lambda i: (0, 0)),
                  _resident((1, d), lambda i: (0, 0))],
        out_specs=pl.BlockSpec((tm, d), row),
        out_shape=jax.ShapeDtypeStruct((m, d), F32),
        compiler_params=_params(1),
        name="ple_gate",
    )(h, p, g, w_gate, w_proj, g_final)


def kernel(x, p, positions, attn_norm, ffn_norm, ple_norm, final_norm, diff_w_qkv, diff_w_o, diff_lambda,
           diff_subln, swa_w_qkv, swa_w_o, swa_sinks, ffn_w_up, ffn_conv_w, ffn_conv_b, ffn_w_down,
           ple_w_proj, ple_w_gate):
    b, s, d = x.shape
    depth = p.shape[0]
    m = b * s
    cos, sin = _rope_tables(positions)
    h = x.reshape(m, d)
    p2 = p.reshape(depth, m, p.shape[-1])
    diff_perm = _pair_perm(2 * DIFF_HEADS)
    swa_q_perm = _pair_perm(SWA_Q_HEADS)
    swa_k_perm = _pair_perm(SWA_KV_HEADS)
    dq = SWA_Q_HEADS * HEAD_DIM
    dkv = SWA_KV_HEADS * HEAD_DIM

    for i in range(depth):
        j = i // 2
        g_attn = attn_norm[i].reshape(1, d)
        if i % 2 == 0:
            w = diff_w_qkv[j]
            w = jnp.concatenate([w[:, :d][:, diff_perm], w[:, d:2 * d][:, diff_perm], w[:, 2 * d:]],
                                axis=1).astype(BF16)
            q, k, v = _diff_qkv(h, g_attn, w, cos, sin)
            lam_init = 0.8 - 0.6 * math.exp(-0.3 * i)
            lam = jnp.zeros((8, LANES), F32).at[:4, :HEAD_DIM].set(diff_lambda[j].astype(F32))
            o = _diff_attention(q.reshape(b, s, d), k.reshape(b, s, d), v.reshape(b, s, d), lam,
                                diff_subln[j].reshape(1, LANES), lam_init)
            h = _oproj(o.reshape(m, d), diff_w_o[j].astype(BF16), h)
        else:
            w = swa_w_qkv[j]
            w = jnp.concatenate([w[:, :dq][:, swa_q_perm], w[:, dq:dq + dkv][:, swa_k_perm], w[:, dq + dkv:]],
                                axis=1).astype(BF16)
            q, ka, kb, va, vb = _swa_qkv(h, g_attn, w, cos, sin)
            sinks = jnp.zeros((1, LANES), F32).at[0, :SWA_Q_HEADS].set(swa_sinks[j].astype(F32) * LOG2E)
            kvw = SWA_KV_HEADS * LANES
            o = _swa_attention(q.reshape(b, s, dq), ka.reshape(b, s, kvw), kb.reshape(b, s, kvw),
                               va.reshape(b, s, kvw), vb.reshape(b, s, kvw), sinks)
            h = _oproj(o.reshape(m, dq), swa_w_o[j].astype(BF16), h)
        cwb = jnp.concatenate([ffn_conv_w[i], ffn_conv_b[i][None, :],
                               jnp.zeros((8 - CONV_WIDTH - 1, ffn_conv_w.shape[-1]), F32)], axis=0)
        h = _ffn(h, ffn_norm[i].reshape(1, d), ffn_w_up[i].astype(BF16), cwb, ffn_w_down[i].astype(BF16), s)
        h = _ple(h, p2, i, ple_norm[i].reshape(1, d), ple_w_gate[i].astype(BF16), ple_w_proj[i].astype(BF16),
                 final_norm.reshape(1, d), final=(i == depth - 1))
    return h.reshape(b, s, d)
```

```python
import functools
import math

import numpy as np
import jax
import jax.numpy as jnp
from jax import lax
from jax.experimental import pallas as pl
from jax.experimental.pallas import tpu as pltpu

F32 = jnp.float32
BF16 = jnp.bfloat16

LANES = 128
HEAD_DIM = 64
HALF = HEAD_DIM // 2
DIFF_HEADS = 8
SWA_Q_HEADS = 16
SWA_KV_HEADS = 4
WINDOW = 128
ROPE_THETA = 10000.0
CONV_WIDTH = 3
EPS = 1e-6
LOG2E = math.log2(math.e)
VMEM_LIMIT_BYTES = 56 * 1024 * 1024


def _params(n_axes):
    return pltpu.CompilerParams(
        dimension_semantics=("arbitrary",) * n_axes,
        vmem_limit_bytes=VMEM_LIMIT_BYTES,
    )


def _resident(shape, index_map):
    return pl.BlockSpec(shape, index_map, pipeline_mode=pl.Buffered(1))


def _rms(x, g):
    return x * lax.rsqrt(jnp.mean(x * x, axis=-1, keepdims=True) + EPS) * g


def _pair_perm(n_heads):
    idx = []
    for j in range(n_heads // 2):
        a, b = 2 * j * HEAD_DIM, (2 * j + 1) * HEAD_DIM
        idx += list(range(a, a + HALF)) + list(range(b, b + HALF))
        idx += list(range(a + HALF, a + HEAD_DIM)) + list(range(b + HALF, b + HEAD_DIM))
    return np.asarray(idx, np.int32)


def _rope_table_kernel(pos_ref, invf_ref, cos_ref, sin_ref):
    ang = pos_ref[...].astype(F32) * invf_ref[...]
    lane = lax.broadcasted_iota(jnp.int32, (1, LANES), 1)
    sign = jnp.where(lane < 2 * HALF, -1.0, 1.0).astype(F32)
    cos_ref[...] = jnp.cos(ang)
    sin_ref[...] = jnp.sin(ang) * sign


def _rope_tables(positions, tm=2048):
    m = positions.size
    tm = min(tm, m)
    inv_freq = ROPE_THETA ** (-jnp.arange(0, HEAD_DIM, 2, dtype=F32) / HEAD_DIM)
    invf = jnp.tile(inv_freq, LANES // HALF).reshape(1, LANES)
    pos = positions.reshape(m, 1)
    return pl.pallas_call(
        _rope_table_kernel,
        grid=(m // tm,),
        in_specs=[pl.BlockSpec((tm, 1), lambda i: (i, 0)),
                  pl.BlockSpec((1, LANES), lambda i: (0, 0))],
        out_specs=[pl.BlockSpec((tm, LANES), lambda i: (i, 0))] * 2,
        out_shape=[jax.ShapeDtypeStruct((m, LANES), F32)] * 2,
        compiler_params=_params(1),
        name="rope_tables",
    )(pos, invf)


def _rope(y, cos, sin):
    return y * cos + pltpu.roll(y, 2 * HALF, axis=1) * sin


def _diff_qkv_kernel(x_ref, g_ref, w_ref, cos_ref, sin_ref, q_ref, k_ref, v_ref, *, d, cw, qscale):
    xn = _rms(x_ref[...], g_ref[...]).astype(BF16)
    cos, sin = cos_ref[...], sin_ref[...]
    for c in range(d // cw):
        sl = slice(c * cw, (c + 1) * cw)
        yq = jnp.dot(xn, w_ref[:, c * cw:(c + 1) * cw], preferred_element_type=F32)
        yk = jnp.dot(xn, w_ref[:, d + c * cw:d + (c + 1) * cw], preferred_element_type=F32)
        for j in range(cw // LANES):
            js = slice(j * LANES, (j + 1) * LANES)
            q_ref[:, c * cw + j * LANES:c * cw + (j + 1) * LANES] = (
                _rope(yq[:, js], cos, sin) * qscale).astype(BF16)
            k_ref[:, c * cw + j * LANES:c * cw + (j + 1) * LANES] = _rope(yk[:, js], cos, sin).astype(BF16)
        v_ref[:, sl] = jnp.dot(xn, w_ref[:, 2 * d + c * cw:2 * d + (c + 1) * cw],
                               preferred_element_type=F32).astype(BF16)


def _diff_qkv(h, g, w, cos, sin, tm=512, cw=512):
    m, d = h.shape
    tm = min(tm, m)
    kern = functools.partial(_diff_qkv_kernel, d=d, cw=cw, qscale=HEAD_DIM ** -0.5 * LOG2E)
    row = lambda i: (i, 0)
    return pl.pallas_call(
        kern,
        grid=(m // tm,),
        in_specs=[pl.BlockSpec((tm, d), row),
                  _resident((1, d), lambda i: (0, 0)),
                  _resident((d, 3 * d), lambda i: (0, 0)),
                  pl.BlockSpec((tm, LANES), row),
                  pl.BlockSpec((tm, LANES), row)],
        out_specs=[pl.BlockSpec((tm, d), row)] * 3,
        out_shape=[jax.ShapeDtypeStruct((m, d), BF16)] * 3,
        compiler_params=_params(1),
        name="diff_qkv",
    )(h, g, w, cos, sin)


def _swa_qkv_kernel(x_ref, g_ref, w_ref, cos_ref, sin_ref, q_ref, ka_ref, kb_ref, va_ref, vb_ref,
                    *, d, cw, qscale):
    xn = _rms(x_ref[...], g_ref[...]).astype(BF16)
    cos, sin = cos_ref[...], sin_ref[...]
    dq = SWA_Q_HEADS * HEAD_DIM
    dkv = SWA_KV_HEADS * HEAD_DIM
    for c in range(dq // cw):
        yq = jnp.dot(xn, w_ref[:, c * cw:(c + 1) * cw], preferred_element_type=F32)
        for j in range(cw // LANES):
            q_ref[:, c * cw + j * LANES:c * cw + (j + 1) * LANES] = (
                _rope(yq[:, j * LANES:(j + 1) * LANES], cos, sin) * qscale).astype(BF16)
    lane = lax.broadcasted_iota(jnp.int32, (1, LANES), 1)
    slot_a = (lane % HEAD_DIM) < HALF
    low = lane < HEAD_DIM
    yk = jnp.dot(xn, w_ref[:, dq:dq + dkv], preferred_element_type=F32)
    yv = jnp.dot(xn, w_ref[:, dq + dkv:dq + 2 * dkv], preferred_element_type=F32)
    for j in range(dkv // LANES):
        kblk = _rope(yk[:, j * LANES:(j + 1) * LANES], cos, sin)
        vblk = yv[:, j * LANES:(j + 1) * LANES]
        k_even = jnp.where(slot_a, kblk, 0.0)
        k_odd = jnp.where(slot_a, 0.0, kblk)
        v_even = jnp.where(low, vblk, 0.0)
        v_odd = jnp.where(low, 0.0, vblk)
        e, o = slice(2 * j * LANES, (2 * j + 1) * LANES), slice((2 * j + 1) * LANES, (2 * j + 2) * LANES)
        ka_ref[:, e] = k_even.astype(BF16)
        kb_ref[:, e] = pltpu.roll(k_even, HALF, axis=1).astype(BF16)
        kb_ref[:, o] = k_odd.astype(BF16)
        ka_ref[:, o] = pltpu.roll(k_odd, LANES - HALF, axis=1).astype(BF16)
        va_ref[:, e] = v_even.astype(BF16)
        vb_ref[:, e] = pltpu.roll(v_even, HEAD_DIM, axis=1).astype(BF16)
        vb_ref[:, o] = v_odd.astype(BF16)
        va_ref[:, o] = pltpu.roll(v_odd, HEAD_DIM, axis=1).astype(BF16)


def _swa_qkv(h, g, w, cos, sin, tm=512, cw=512):
    m, d = h.shape
    tm = min(tm, m)
    dq = SWA_Q_HEADS * HEAD_DIM
    n = w.shape[1]
    kvw = SWA_KV_HEADS * LANES
    kern = functools.partial(_swa_qkv_kernel, d=d, cw=cw, qscale=HEAD_DIM ** -0.5 * LOG2E)
    row = lambda i: (i, 0)
    return pl.pallas_call(
        kern,
        grid=(m // tm,),
        in_specs=[pl.BlockSpec((tm, d), row),
                  _resident((1, d), lambda i: (0, 0)),
                  _resident((d, n), lambda i: (0, 0)),
                  pl.BlockSpec((tm, LANES), row),
                  pl.BlockSpec((tm, LANES), row)],
        out_specs=[pl.BlockSpec((tm, dq), row)] + [pl.BlockSpec((tm, kvw), row)] * 4,
        out_shape=[jax.ShapeDtypeStruct((m, dq), BF16)] + [jax.ShapeDtypeStruct((m, kvw), BF16)] * 4,
        compiler_params=_params(1),
        name="swa_qkv",
    )(h, g, w, cos, sin)


def _diff_attn_kernel(q_ref, k_ref, v_ref, lam_ref, sub_ref, o_ref, q2_ref, s0_ref, s1_ref, mx0_ref, mx1_ref,
                      m_ref, l_ref, acc_ref, *, tq, tk, nt, seq, lam_init):
    lane = lax.broadcasted_iota(jnp.int32, (1, LANES), 1)
    slot_a = (lane % HEAD_DIM) < HALF
    q = q_ref[0]
    zero = jnp.zeros_like(q)
    q2_ref[0:tq, :] = jnp.where(slot_a, q, zero)
    q2_ref[tq:2 * tq, :] = jnp.where(slot_a, zero, q)
    m_ref[...] = jnp.full(m_ref.shape, -jnp.inf, F32)
    l_ref[...] = jnp.zeros(l_ref.shape, F32)
    acc_ref[...] = jnp.zeros(acc_ref.shape, F32)
    col_tiles = [slice(n * nt, (n + 1) * nt) for n in range(2 * tq // nt)]

    def scores(kc, s_ref, mx_ref):
        k = k_ref[0, pl.ds(pl.multiple_of(kc * tk, tk), tk), :]
        for cs in col_tiles:
            st = lax.dot_general(k, q2_ref[cs, :], (((1,), (1,)), ((), ())),
                                 preferred_element_type=F32)
            s_ref[:, cs] = st
            mx_ref[:, cs] = jnp.max(st, axis=0, keepdims=True)

    def consume(kc, s_ref, mx_ref):
        v = v_ref[0, pl.ds(pl.multiple_of(kc * tk, tk), tk), :]
        for cs in col_tiles:
            m_old = m_ref[:, cs]
            m_new = jnp.maximum(m_old, mx_ref[:, cs])
            pt = jnp.exp2(s_ref[:, cs] - m_new)
            alpha = jnp.exp2(m_old - m_new)
            l_ref[:, cs] = alpha * l_ref[:, cs] + jnp.sum(pt, axis=0, keepdims=True)
            pv = lax.dot_general(v, pt.astype(BF16), (((0,), (0,)), ((), ())),
                                 preferred_element_type=F32)
            acc_ref[:, cs] = alpha * acc_ref[:, cs] + pv
            m_ref[:, cs] = m_new

    n_chunks = seq // tk
    scores(0, s0_ref, mx0_ref)

    def pair(pi, carry):
        scores(2 * pi + 1, s1_ref, mx1_ref)
        consume(2 * pi, s0_ref, mx0_ref)
        scores(2 * pi + 2, s0_ref, mx0_ref)
        consume(2 * pi + 1, s1_ref, mx1_ref)
        return carry

    lax.fori_loop(0, n_chunks // 2 - 1, pair, 0)
    scores(n_chunks - 1, s1_ref, mx1_ref)
    consume(n_chunks - 2, s0_ref, mx0_ref)
    consume(n_chunks - 1, s1_ref, mx1_ref)

    lam = lam_ref[...]
    lam_full = (jnp.exp(jnp.sum(lam[0:1] * lam[1:2], axis=1, keepdims=True))
                - jnp.exp(jnp.sum(lam[2:3] * lam[3:4], axis=1, keepdims=True)) + lam_init)
    ot_all = acc_ref[...] * (1.0 / l_ref[...])
    ot = ot_all[:, :tq] - lam_full * ot_all[:, tq:]
    ot = ot * lax.rsqrt(jnp.mean(ot * ot, axis=0, keepdims=True) + EPS)
    o_ref[0] = (ot.T * sub_ref[...] * (1.0 - lam_init)).astype(o_ref.dtype)


def _diff_attention(q, k, v, lam, subln, lam_init, tq=1024, tk=512, nt=256):
    b, s, d = q.shape
    tq, tk = min(tq, s), min(tk, s // 2)
    assert s % (2 * tk) == 0 and s % tq == 0 and (2 * tq) % nt == 0
    kern = functools.partial(_diff_attn_kernel, tq=tq, tk=tk, nt=nt, seq=s, lam_init=lam_init)
    return pl.pallas_call(
        kern,
        grid=(b, d // LANES, s // tq),
        in_specs=[pl.BlockSpec((1, tq, LANES), lambda bi, h, i: (bi, i, h)),
                  pl.BlockSpec((1, s, LANES), lambda bi, h, i: (bi, 0, h)),
                  pl.BlockSpec((1, s, LANES), lambda bi, h, i: (bi, 0, h)),
                  _resident((8, LANES), lambda bi, h, i: (0, 0)),
                  _resident((1, LANES), lambda bi, h, i: (0, 0))],
        out_specs=pl.BlockSpec((1, tq, LANES), lambda bi, h, i: (bi, i, h)),
        out_shape=jax.ShapeDtypeStruct((b, s, d), BF16),
        scratch_shapes=[pltpu.VMEM((2 * tq, LANES), BF16),
                        pltpu.VMEM((tk, 2 * tq), F32), pltpu.VMEM((tk, 2 * tq), F32),
                        pltpu.VMEM((1, 2 * tq), F32), pltpu.VMEM((1, 2 * tq), F32),
                        pltpu.VMEM((1, 2 * tq), F32), pltpu.VMEM((1, 2 * tq), F32),
                        pltpu.VMEM((LANES, 2 * tq), F32)],
        compiler_params=_params(3),
        name="diff_attention",
    )(q, k, v, lam, subln)


def _swa_attn_kernel(q_ref, ka_ref, kb_ref, va_ref, vb_ref, sink_ref, o_ref, *, tq, band, seq):
    i = pl.program_id(1)
    start = jnp.clip(i * tq - WINDOW, 0, seq - band)
    start = pl.multiple_of(start, WINDOW)
    pairs = SWA_Q_HEADS // SWA_KV_HEADS // 2
    ncol = pairs * tq
    kpos = start + lax.broadcasted_iota(jnp.int32, (band, ncol), 0)
    qpos = i * tq + lax.broadcasted_iota(jnp.int32, (band, ncol), 1) % tq
    mask = jnp.abs(kpos - qpos) <= WINDOW
    chains = []
    for g in range(SWA_KV_HEADS):
        gl = slice(g * LANES, (g + 1) * LANES)
        qg = jnp.concatenate([q_ref[0, :, (g * pairs + t) * LANES:(g * pairs + t + 1) * LANES]
                              for t in range(pairs)], axis=0)
        for kk_ref, vv_ref, which in ((ka_ref, va_ref, 0), (kb_ref, vb_ref, 1)):
            kk = kk_ref[0, pl.ds(start, band), gl]
            st = lax.dot_general(kk, qg, (((1,), (1,)), ((), ())), preferred_element_type=F32)
            chains.append((g, vv_ref, which, st))
    outs = {}
    for g, vv_ref, which, st in chains:
        gl = slice(g * LANES, (g + 1) * LANES)
        vv = vv_ref[0, pl.ds(start, band), gl]
        heads = [g * 2 * pairs + 2 * t + which for t in range(pairs)]
        sink = jnp.concatenate([jnp.broadcast_to(sink_ref[h:h + 1, :], (1, tq)) for h in heads], axis=1)
        st = jnp.where(mask, st, -jnp.inf)
        m = jnp.maximum(jnp.max(st, axis=0, keepdims=True), sink)
        e = jnp.exp2(st - m)
        denom = jnp.sum(e, axis=0, keepdims=True) + jnp.exp2(sink - m)
        ot = lax.dot_general(vv, e.astype(BF16), (((0,), (0,)), ((), ())),
                             preferred_element_type=F32)
        ot = ot * (1.0 / denom)
        outs[g] = ot if g not in outs else outs[g] + ot
    for g in range(SWA_KV_HEADS):
        for t in range(pairs):
            blk = g * pairs + t
            o_ref[0, :, blk * LANES:(blk + 1) * LANES] = outs[g][:, t * tq:(t + 1) * tq].T.astype(o_ref.dtype)


def _swa_attention(q, ka, kb, va, vb, sinks, tq=128):
    b, s, dq = q.shape
    kvw = ka.shape[-1]
    tq = min(tq, s)
    assert tq == LANES
    band = min(tq + 2 * WINDOW, s)
    kern = functools.partial(_swa_attn_kernel, tq=tq, band=band, seq=s)
    full = pl.BlockSpec((1, s, kvw), lambda bi, i: (bi, 0, 0))
    return pl.pallas_call(
        kern,
        grid=(b, s // tq),
        in_specs=[pl.BlockSpec((1, tq, dq), lambda bi, i: (bi, i, 0)), full, full, full, full,
                  _resident((SWA_Q_HEADS, LANES), lambda bi, i: (0, 0))],
        out_specs=pl.BlockSpec((1, tq, dq), lambda bi, i: (bi, i, 0)),
        out_shape=jax.ShapeDtypeStruct((b, s, dq), BF16),
        compiler_params=_params(2),
        name="swa_attention",
    )(q, ka, kb, va, vb, sinks)


def _oproj_kernel(o_ref, w_ref, h_ref, out_ref):
    out_ref[...] = h_ref[...] + jnp.dot(o_ref[...], w_ref[...], preferred_element_type=F32)


def _oproj(o, w, h, tm=1024):
    m, d = h.shape
    tm = min(tm, m)
    row = lambda i: (i, 0)
    return pl.pallas_call(
        _oproj_kernel,
        grid=(m // tm,),
        in_specs=[pl.BlockSpec((tm, o.shape[1]), row),
                  _resident(w.shape, lambda i: (0, 0)),
                  pl.BlockSpec((tm, d), row)],
        out_specs=pl.BlockSpec((tm, d), row),
        out_shape=jax.ShapeDtypeStruct((m, d), F32),
        compiler_params=_params(1),
        name="out_proj",
    )(o, w, h)


def _ffn_kernel(x_ref, xp_ref, xn_ref, g_ref, wup_ref, cwb_ref, wdn_ref, out_ref, *, tm, halo, seq, tf, dff):
    i = pl.program_id(0)
    g = g_ref[...]
    x = x_ref[...]
    row0 = i * tm
    prev_ok = (row0 % seq) != 0
    next_ok = ((row0 + tm) % seq) != 0
    hp = jnp.where(prev_ok, _rms(xp_ref[...], g), 0.0)
    hn = jnp.where(next_ok, _rms(xn_ref[...], g), 0.0)
    xe = jnp.concatenate([hp, _rms(x, g), hn], axis=0).astype(BF16)
    rows = tm + 2 * halo

    def conv(u, col):
        w = cwb_ref[:, col:col + tf]
        prev = pltpu.roll(u, 1, axis=0)[halo:halo + tm]
        nxt = pltpu.roll(u, rows - 1, axis=0)[halo:halo + tm]
        return prev * w[0:1] + u[halo:halo + tm] * w[1:2] + nxt * w[2:3] + w[3:4]

    def up(c):
        return (jnp.dot(xe, wup_ref[:, c * tf:(c + 1) * tf], preferred_element_type=F32),
                jnp.dot(xe, wup_ref[:, dff + c * tf:dff + (c + 1) * tf], preferred_element_type=F32))

    n_chunks = dff // tf
    acc = x
    nxt = up(0)
    for c in range(n_chunks):
        ug, uv = nxt
        if c + 1 < n_chunks:
            nxt = up(c + 1)
        gate = conv(ug, c * tf)
        val = conv(uv, dff + c * tf)
        act = (gate * jax.nn.sigmoid(gate) * val).astype(BF16)
        acc = acc + jnp.dot(act, wdn_ref[c * tf:(c + 1) * tf, :], preferred_element_type=F32)
    out_ref[...] = acc


def _ffn(h, g, w_up, cwb, w_down, seq, tm=512, tf=256):
    m, d = h.shape
    tm = min(tm, seq)
    halo = 8
    dff = w_down.shape[0]
    nhb = m // halo
    kern = functools.partial(_ffn_kernel, tm=tm, halo=halo, seq=seq, tf=tf, dff=dff)
    row = lambda i: (i, 0)
    return pl.pallas_call(
        kern,
        grid=(m // tm,),
        in_specs=[pl.BlockSpec((tm, d), row),
                  pl.BlockSpec((halo, d), lambda i: (jnp.maximum(i * (tm // halo) - 1, 0), 0)),
                  pl.BlockSpec((halo, d), lambda i: (jnp.minimum((i + 1) * (tm // halo), nhb - 1), 0)),
                  _resident((1, d), lambda i: (0, 0)),
                  _resident(w_up.shape, lambda i: (0, 0)),
                  _resident(cwb.shape, lambda i: (0, 0)),
                  _resident(w_down.shape, lambda i: (0, 0))],
        out_specs=pl.BlockSpec((tm, d), row),
        out_shape=jax.ShapeDtypeStruct((m, d), F32),
        compiler_params=_params(1),
        name="conv_ffn",
    )(h, h, h, g, w_up, cwb, w_down)


def _ple_kernel(h_ref, p_ref, g_ref, wg_ref, wp_ref, gf_ref, out_ref, *, final):
    h = h_ref[...]
    hn = _rms(h, g_ref[...]).astype(BF16)
    gate = jax.nn.sigmoid(jnp.dot(hn, wg_ref[...], preferred_element_type=F32))
    proj = jnp.dot(p_ref[0].astype(BF16), wp_ref[...], preferred_element_type=F32)
    out = h + proj * gate
    if final:
        out = _rms(out, gf_ref[...])
    out_ref[...] = out


def _ple(h, p, layer, g, w_gate, w_proj, g_final, final, tm=1024):
    m, d = h.shape
    tm = min(tm, m)
    row = lambda i: (i, 0)
    return pl.pallas_call(
        functools.partial(_ple_kernel, final=final),
        grid=(m // tm,),
        in_specs=[pl.BlockSpec((tm, d), row),
                  pl.BlockSpec((1, tm, p.shape[-1]), lambda i: (layer, i, 0)),
                  _resident((1, d), lambda i: (0, 0)),
                  _resident(w_gate.shape, lambda i: (0, 0)),
                  _resident(w_proj.shape, lambda i: (0, 0)),
                  _resident((1, d), lambda i: (0, 0))],
        out_specs=pl.BlockSpec((tm, d), row),
        out_shape=jax.ShapeDtypeStruct((m, d), F32),
        compiler_params=_params(1),
        name="ple_gate",
    )(h, p, g, w_gate, w_proj, g_final)


def kernel(x, p, positions, attn_norm, ffn_norm, ple_norm, final_norm, diff_w_qkv, diff_w_o, diff_lambda,
           diff_subln, swa_w_qkv, swa_w_o, swa_sinks, ffn_w_up, ffn_conv_w, ffn_conv_b, ffn_w_down,
           ple_w_proj, ple_w_gate):
    b, s, d = x.shape
    depth = p.shape[0]
    m = b * s
    cos, sin = _rope_tables(positions)
    h = x.reshape(m, d)
    p2 = p.reshape(depth, m, p.shape[-1])
    diff_perm = _pair_perm(2 * DIFF_HEADS)
    swa_q_perm = _pair_perm(SWA_Q_HEADS)
    swa_k_perm = _pair_perm(SWA_KV_HEADS)
    dq = SWA_Q_HEADS * HEAD_DIM
    dkv = SWA_KV_HEADS * HEAD_DIM

    for i in range(depth):
        j = i // 2
        g_attn = attn_norm[i].reshape(1, d)
        if i % 2 == 0:
            w = diff_w_qkv[j]
            w = jnp.concatenate([w[:, :d][:, diff_perm], w[:, d:2 * d][:, diff_perm], w[:, 2 * d:]],
                                axis=1).astype(BF16)
            q, k, v = _diff_qkv(h, g_attn, w, cos, sin)
            lam_init = 0.8 - 0.6 * math.exp(-0.3 * i)
            lam = jnp.zeros((8, LANES), F32).at[:4, :HEAD_DIM].set(diff_lambda[j].astype(F32))
            o = _diff_attention(q.reshape(b, s, d), k.reshape(b, s, d), v.reshape(b, s, d), lam,
                                diff_subln[j].reshape(1, LANES), lam_init)
            h = _oproj(o.reshape(m, d), diff_w_o[j].astype(BF16), h)
        else:
            w = swa_w_qkv[j]
            w = jnp.concatenate([w[:, :dq][:, swa_q_perm], w[:, dq:dq + dkv][:, swa_k_perm], w[:, dq + dkv:]],
                                axis=1).astype(BF16)
            q, ka, kb, va, vb = _swa_qkv(h, g_attn, w, cos, sin)
            sinks = jnp.broadcast_to((swa_sinks[j].astype(F32) * LOG2E)[:, None], (SWA_Q_HEADS, LANES))
            kvw = SWA_KV_HEADS * LANES
            o = _swa_attention(q.reshape(b, s, dq), ka.reshape(b, s, kvw), kb.reshape(b, s, kvw),
                               va.reshape(b, s, kvw), vb.reshape(b, s, kvw), sinks)
            h = _oproj(o.reshape(m, dq), swa_w_o[j].astype(BF16), h)
        cwb = jnp.concatenate([ffn_conv_w[i], ffn_conv_b[i][None, :],
                               jnp.zeros((8 - CONV_WIDTH - 1, ffn_conv_w.shape[-1]), F32)], axis=0)
        h = _ffn(h, ffn_norm[i].reshape(1, d), ffn_w_up[i].astype(BF16), cwb, ffn_w_down[i].astype(BF16), s)
        h = _ple(h, p2, i, ple_norm[i].reshape(1, d), ple_w_gate[i].astype(BF16), ple_w_proj[i].astype(BF16),
                 final_norm.reshape(1, d), final=(i == depth - 1))
    return h.reshape(b, s, d)
```
